```python
import math
import jax, jax.numpy as jnp
from jax import lax
import numpy as np

D_MODEL = 1024
BATCH = 16
SEQ = 2048
DEPTH = 2

CHUNK = 64
N_MIXERS = 2
N_MLSTM = (DEPTH + 1) // 2
N_DIFF = DEPTH // 2
M_HEADS = 4
M_DK = D_MODEL // M_HEADS
M_DV = D_MODEL // M_HEADS
M_CONV = 4
M_IN_COLS = 2 * M_HEADS * M_DK + 2 * M_HEADS * M_DV + 2 * M_HEADS
D_HEADS = 8
D_HD = D_MODEL // (2 * D_HEADS)
D_VD = 2 * D_HD
D_IN_COLS = 3 * D_HEADS * D_VD
Q_BLOCK = 128
REL_BUCKETS = 32
REL_MAX_DIST = 128
D_FF = 2816
FFN_CONV = 3
LN_EPS = 1e-5
DEEPNORM_ALPHA = (2.0 * DEPTH) ** 0.25
DEEPNORM_BETA = (8.0 * DEPTH) ** -0.25

kernel_name = "hybrid_mlstm_diffattn_convffn_trunk"


def _layer_norm(x, g, b):
    xf = x.astype(jnp.float32)
    mu = jnp.mean(xf, axis=-1, keepdims=True)
    var = jnp.mean(jnp.square(xf - mu), axis=-1, keepdims=True)
    y = (xf - mu) * lax.rsqrt(var + LN_EPS) * g.astype(jnp.float32) + b.astype(jnp.float32)
    return y.astype(x.dtype)


def _rms_heads(h, g, n_heads):
    B, S, _ = h.shape
    hh = h.reshape(B, S, n_heads, -1)
    hh = hh * lax.rsqrt(jnp.mean(hh * hh, axis=-1, keepdims=True) + LN_EPS)
    return hh.reshape(B, S, -1) * g.astype(jnp.float32)


def _causal_dwconv(x, w, b):
    W = w.shape[0]
    S = x.shape[1]
    xp = jnp.pad(x, ((0, 0), (W - 1, 0), (0, 0)))
    y = b
    for j in range(W):
        y = y + xp[:, j:j + S, :] * w[j]
    return y


def _t5_bucket(rel):
    nb = REL_BUCKETS // 2
    max_exact = nb // 2
    n = -rel
    ret = jnp.where(n < 0, nb, 0)
    n = jnp.abs(n)
    nf = jnp.maximum(n, 1).astype(jnp.float32)
    large = max_exact + (jnp.log(nf / max_exact) / math.log(REL_MAX_DIST / max_exact)
                         * (nb - max_exact)).astype(jnp.int32)
    large = jnp.minimum(large, nb - 1)
    return ret + jnp.where(n < max_exact, n, large)


def _lambda_init(layer_idx):
    return 0.8 - 0.6 * math.exp(-0.3 * layer_idx)


def _mlstm_mixer(u, w_in, b_gate, conv_w, conv_b, norm_g, w_out):
    B, S, _ = u.shape
    H, L = M_HEADS, CHUNK
    NC = S // L
    f32 = jnp.float32
    proj = u @ w_in
    s1 = 2 * H * M_DK
    s2 = s1 + H * M_DV
    s3 = s2 + H * M_DV
    qk, v, o, gates = jnp.split(proj, [s1, s2, s3], axis=-1)
    qk = jax.nn.silu(_causal_dwconv(qk, conv_w, conv_b))
    q, k = jnp.split(qk, 2, axis=-1)
    gates = (gates + b_gate).astype(f32)
    ig = gates[..., :H]
    lf = jax.nn.log_sigmoid(gates[..., H:])

    def to_chunks(t, d):
        return t.astype(f32).reshape(B, NC, L, H, d).transpose(1, 0, 3, 2, 4)

    qc = to_chunks(q, M_DK)
    kc = to_chunks(k, M_DK) * (M_DK ** -0.5)
    vc = to_chunks(v, M_DV)
    igc = ig.reshape(B, NC, L, H).transpose(1, 0, 3, 2)
    lfc = lf.reshape(B, NC, L, H).transpose(1, 0, 3, 2)
    tril = jnp.tril(jnp.ones((L, L), dtype=bool))

    def step(carry, inp):
        C, n, m = carry
        qb, kb, vb, ib, fb = inp
        bcum = jnp.cumsum(fb, axis=-1)
        dmat = jnp.where(tril, bcum[..., :, None] - bcum[..., None, :] + ib[..., None, :], -jnp.inf)
        inter = bcum + m[..., None]
        m_t = jnp.maximum(inter, jnp.max(dmat, axis=-1))
        w = jnp.exp(dmat - m_t[..., None])
        s_inter = jnp.exp(inter - m_t)
        a = jnp.einsum('bhtd,bhsd->bhts', qb, kb) * w
        num = s_inter[..., None] * jnp.einsum('bhtd,bhde->bhte', qb, C) + jnp.einsum('bhts,bhse->bhte', a, vb)
        den = s_inter * jnp.einsum('bhtd,bhd->bht', qb, n) + jnp.sum(a, axis=-1)
        h = num / jnp.maximum(jnp.abs(den), jnp.exp(-m_t))[..., None]
        b_last = bcum[..., -1]
        g = b_last[..., None] - bcum + ib
        m_new = jnp.maximum(b_last + m, jnp.max(g, axis=-1))
        decay = jnp.exp(b_last + m - m_new)
        w_s = jnp.exp(g - m_new[..., None])
        C_new = decay[..., None, None] * C + jnp.einsum('bhsd,bhse->bhde', kb * w_s[..., None], vb)
        n_new = decay[..., None] * n + jnp.einsum('bhs,bhsd->bhd', w_s, kb)
        return (C_new, n_new, m_new), h

    init = (jnp.zeros((B, H, M_DK, M_DV), f32), jnp.zeros((B, H, M_DK), f32), jnp.zeros((B, H), f32))
    _, hc = lax.scan(step, init, (qc, kc, vc, igc, lfc))
    h = hc.transpose(1, 0, 3, 2, 4).reshape(B, S, H * M_DV)
    h = _rms_heads(h, norm_g, H) * jax.nn.sigmoid(o.astype(f32))
    return h.astype(u.dtype) @ w_out


def _diff_attn_mixer(u, w_in, lam, norm_g, w_out, rel_bias, lambda_init):
    B, S, _ = u.shape
    H = D_HEADS
    f32 = jnp.float32
    proj = u @ w_in
    q, k, v = jnp.split(proj, 3, axis=-1)
    q = q.astype(f32).reshape(B, S, H, 2, D_HD).transpose(0, 2, 3, 1, 4) * (D_HD ** -0.5)
    k = k.astype(f32).reshape(B, S, H, 2, D_HD).transpose(0, 2, 3, 1, 4)
    v = v.astype(f32).reshape(B, S, H, D_VD).transpose(0, 2, 1, 3)
    lam = lam.astype(f32)
    lam_full = jnp.exp(jnp.sum(lam[0] * lam[1])) - jnp.exp(jnp.sum(lam[2] * lam[3])) + lambda_init
    table = rel_bias.astype(f32)
    kpos = jnp.arange(S)

    def block(j):
        q0 = j * Q_BLOCK
        qb = lax.dynamic_slice_in_dim(q, q0, Q_BLOCK, axis=3)
        qpos = q0 + jnp.arange(Q_BLOCK)
        bias = table[_t5_bucket(kpos[None, :] - qpos[:, None])].transpose(2, 0, 1)
        allowed = (kpos[None, :] // CHUNK) <= (qpos[:, None] // CHUNK)
        logits = jnp.einsum('bhcqd,bhckd->bhcqk', qb, k) + bias[None, :, None]
        logits = jnp.where(allowed, logits, -jnp.inf)
        p = jax.nn.softmax(logits, axis=-1)
        pdiff = p[:, :, 0] - lam_full * p[:, :, 1]
        return jnp.einsum('bhqk,bhkd->bhqd', pdiff, v)

    out = lax.map(block, jnp.arange(S // Q_BLOCK))
    out = out.transpose(1, 0, 3, 2, 4).reshape(B, S, H * D_VD)
    out = _rms_heads(out, norm_g, H) * (1.0 - lambda_init)
    return out.astype(u.dtype) @ w_out


def _conv_ffn(u, w_up, conv_w, conv_b, w_down):
    h = _causal_dwconv(u @ w_up, conv_w, conv_b)
    a, g = jnp.split(h, 2, axis=-1)
    return (jax.nn.gelu(g, approximate=False) * a) @ w_down


def setup_inputs(seed: int = 0) -> dict:
    key = jax.random.key(seed)
    ks = jax.random.split(key, 24)
    f32 = jnp.float32
    nrm = lambda k, shape, s: jax.random.normal(k, shape, f32) * s
    D = D_MODEL
    b_i = nrm(ks[7], (N_MLSTM, M_HEADS), 0.1)
    b_f = jnp.linspace(3.0, 6.0, M_HEADS, dtype=f32)[None, :] + nrm(ks[8], (N_MLSTM, M_HEADS), 0.1)
    return {
        "x": nrm(ks[0], (BATCH, SEQ, D), 1.0),
        "c": nrm(ks[1], (BATCH, D), 1.0),
        "w_ada": nrm(ks[2], (DEPTH, D, 6 * D), 0.1 * D ** -0.5),
        "b_ada": nrm(ks[3], (DEPTH, 6 * D), 0.02),
        "ln_g": 1.0 + nrm(ks[4], (DEPTH, 2, D), 0.02),
        "ln_b": nrm(ks[5], (DEPTH, 2, D), 0.02),
        "m_w_in": nrm(ks[6], (N_MLSTM, D, M_IN_COLS), D ** -0.5),
        "m_b_gate": jnp.concatenate([b_i, b_f], axis=-1),
        "m_conv_w": nrm(ks[9], (N_MLSTM, M_CONV, 2 * M_HEADS * M_DK), M_CONV ** -0.5),
        "m_conv_b": nrm(ks[10], (N_MLSTM, 2 * M_HEADS * M_DK), 0.02),
        "m_norm_g": 1.0 + nrm(ks[11], (N_MLSTM, M_HEADS * M_DV), 0.02),
        "m_w_out": nrm(ks[12], (N_MLSTM, M_HEADS * M_DV, D), DEEPNORM_BETA * (M_HEADS * M_DV) ** -0.5),
        "d_w_in": nrm(ks[13], (N_DIFF, D, D_IN_COLS), D ** -0.5),
        "d_lambda": nrm(ks[14], (N_DIFF, 4, D_HD), 0.1),
        "d_norm_g": 1.0 + nrm(ks[15], (N_DIFF, D_HEADS * D_VD), 0.02),
        "d_w_out": nrm(ks[16], (N_DIFF, D_HEADS * D_VD, D), DEEPNORM_BETA * (D_HEADS * D_VD) ** -0.5),
        "rel_bias": nrm(ks[17], (REL_BUCKETS, D_HEADS), 0.3),
        "f_w_up": nrm(ks[18], (DEPTH, D, 2 * D_FF), D ** -0.5),
        "f_conv_w": nrm(ks[19], (DEPTH, FFN_CONV, 2 * D_FF), FFN_CONV ** -0.5),
        "f_conv_b": nrm(ks[20], (DEPTH, 2 * D_FF), 0.02),
        "f_w_down": nrm(ks[21], (DEPTH, D_FF, D), DEEPNORM_BETA * D_FF ** -0.5),
    }


def reference(x, c, w_ada, b_ada, ln_g, ln_b, m_w_in, m_b_gate, m_conv_w, m_conv_b, m_norm_g, m_w_out,
              d_w_in, d_lambda, d_norm_g, d_w_out, rel_bias, f_w_up, f_conv_w, f_conv_b, f_w_down):
    cs = jax.nn.silu(c)
    for i in range(DEPTH):
        ada = cs @ w_ada[i] + b_ada[i]
        sh1, sc1, g1, sh2, sc2, g2 = [t[:, None, :] for t in jnp.split(ada, 6, axis=-1)]
        u = x * (1.0 + sc1) + sh1
        j = i // N_MIXERS
        if i % N_MIXERS == 0:
            y = _mlstm_mixer(u, m_w_in[j], m_b_gate[j], m_conv_w[j], m_conv_b[j], m_norm_g[j], m_w_out[j])
        else:
            y = _diff_attn_mixer(u, d_w_in[j], d_lambda[j], d_norm_g[j], d_w_out[j], rel_bias, _lambda_init(i))
        x = _layer_norm(DEEPNORM_ALPHA * x + (1.0 + g1) * y, ln_g[i, 0], ln_b[i, 0])
        u = x * (1.0 + sc2) + sh2
        y = _conv_ffn(u, f_w_up[i], f_conv_w[i], f_conv_b[i], f_w_down[i])
        x = _layer_norm(DEEPNORM_ALPHA * x + (1.0 + g2) * y, ln_g[i, 1], ln_b[i, 1])
    return x
```

```python
import functools
import math

import numpy as np
import jax
import jax.numpy as jnp
from jax import lax
from jax.experimental import pallas as pl
from jax.experimental.pallas import tpu as pltpu

F32 = jnp.float32
BF16 = jnp.bfloat16

D_MODEL = 1024
BATCH = 16
SEQ = 2048
DEPTH = 2
CHUNK = 64
M_HEADS = 4
M_DK = 256
M_DV = 256
M_CONV = 4
D_HEADS = 8
D_HD = 64
D_VD = 128
REL_BUCKETS = 32
REL_MAX_DIST = 128
D_FF = 2816
FFN_CONV = 3
LN_EPS = 1e-5
DEEPNORM_ALPHA = (2.0 * DEPTH) ** 0.25

N_TOK = BATCH * SEQ
LANES = 128
SUBLANES = 8
VMEM_LIMIT = 56 * 2 ** 20

ROW_TILE = 512
MLSTM_L = 128
ATT_TQ = 256
ATT_TK = 256
FFN_FC = 256
FFN_NCH = D_FF // FFN_FC
NEG_BIG = -1e30


def _const_spec(shape):
    nd = len(shape)
    return pl.BlockSpec(shape, lambda *_: (0,) * nd, pipeline_mode=pl.Buffered(1))


def _sigmoid(x):
    return 1.0 / (1.0 + jnp.exp(-x))


def _log_sigmoid(x):
    return jnp.minimum(x, 0.0) - jnp.log1p(jnp.exp(-jnp.abs(x)))


def _causal_conv(p, carry, buf_ref, w, b, width):
    t = p.shape[0]
    buf_ref[0:SUBLANES, :] = carry
    buf_ref[SUBLANES:SUBLANES + t, :] = p
    y = b + w[width - 1:width, :] * p
    for j in range(width - 1):
        back = width - 1 - j
        y = y + w[j:j + 1, :] * buf_ref[pl.ds(SUBLANES - back, t), :]
    return y


def _ada_kernel(c_ref, w_ref, b_ref, o_ref):
    c = c_ref[...]
    cs = c * _sigmoid(c)
    o_ref[0] = jnp.dot(cs, w_ref[0], precision=lax.Precision.HIGHEST,
                       preferred_element_type=F32) + b_ref[0]


def _ada_call(c, w_ada, b_ada):
    d = D_MODEL
    return pl.pallas_call(
        _ada_kernel,
        grid=(DEPTH, 6),
        in_specs=[
            pl.BlockSpec((BATCH, d), lambda i, j: (0, 0)),
            pl.BlockSpec((1, d, d), lambda i, j: (i, 0, j)),
            pl.BlockSpec((1, 1, d), lambda i, j: (i, 0, j)),
        ],
        out_specs=pl.BlockSpec((1, BATCH, d), lambda i, j: (i, 0, j)),
        out_shape=jax.ShapeDtypeStruct((DEPTH, BATCH, 6 * d), F32),
        name="ada_ln",
    )(c, w_ada, b_ada.reshape(DEPTH, 1, 6 * d))


def _mlstm_inproj_kernel(x_ref, ada_ref, wqk_ref, wvo_ref, wg_ref, bg_ref, cw_ref, cb_ref,
                         qkvo_ref, gcol_ref, grow_ref, carry_ref, buf_ref):
    d = D_MODEL
    tm = x_ref.shape[0]
    cc = 512

    @pl.when(pl.program_id(1) == 0)
    def _():
        carry_ref[...] = jnp.zeros_like(carry_ref)

    sh = ada_ref[:, 0:d]
    sc = ada_ref[:, d:2 * d]
    u = (x_ref[...] * (1.0 + sc) + sh).astype(BF16)
    for c in range(2 * d // cc):
        cols = slice(c * cc, (c + 1) * cc)
        p = jnp.dot(u, wqk_ref[:, cols], preferred_element_type=F32)
        y = _causal_conv(p, carry_ref[:, cols], buf_ref, cw_ref[:, cols], cb_ref[:, cols], M_CONV)
        carry_ref[:, cols] = p[tm - SUBLANES:tm, :]
        y = y * _sigmoid(y)
        if c * cc >= d:
            y = y * (M_DK ** -0.5)
        qkvo_ref[:, cols] = y.astype(BF16)
    for c in range(2 * d // cc):
        cols = slice(c * cc, (c + 1) * cc)
        p = jnp.dot(u, wvo_ref[:, cols], preferred_element_type=F32)
        qkvo_ref[:, 2 * d + c * cc:2 * d + (c + 1) * cc] = p.astype(BF16)
    g = jnp.dot(u, wg_ref[...], preferred_element_type=F32) + bg_ref[...]
    gcol_ref[...] = g
    grow_ref[0] = g.T[0:SUBLANES, :]


def _mlstm_inproj_call(x2, ada_l, wqk, wvo, wg, bg, cw, cb):
    d = D_MODEL
    tm = ROW_TILE
    nt = SEQ // tm
    return pl.pallas_call(
        _mlstm_inproj_kernel,
        grid=(BATCH, nt),
        in_specs=[
            pl.BlockSpec((tm, d), lambda b, s: (b * nt + s, 0)),
            pl.BlockSpec((None, 1, 6 * d), lambda b, s: (b, 0, 0)),
            _const_spec((d, 2 * d)),
            _const_spec((d, 2 * d)),
            _const_spec((d, LANES)),
            _const_spec((1, LANES)),
            _const_spec((M_CONV, 2 * d)),
            _const_spec((1, 2 * d)),
        ],
        out_specs=[
            pl.BlockSpec((tm, 4 * d), lambda b, s: (b * nt + s, 0)),
            pl.BlockSpec((tm, LANES), lambda b, s: (b * nt + s, 0)),
            pl.BlockSpec((1, SUBLANES, tm), lambda b, s: (b, 0, s)),
        ],
        out_shape=[
            jax.ShapeDtypeStruct((N_TOK, 4 * d), BF16),
            jax.ShapeDtypeStruct((N_TOK, LANES), F32),
            jax.ShapeDtypeStruct((BATCH, SUBLANES, SEQ), F32),
        ],
        name="mlstm_inproj",
        scratch_shapes=[
            pltpu.VMEM((SUBLANES, 2 * d), F32),
            pltpu.VMEM((tm + SUBLANES, 512), F32),
        ],
        compiler_params=pltpu.CompilerParams(
            dimension_semantics=("arbitrary", "arbitrary"), vmem_limit_bytes=VMEM_LIMIT),
    )(x2, ada_l, wqk, wvo, wg, bg, cw, cb)


def _mlstm_core_kernel(q_ref, k_ref, v_ref, o_ref, gcol_ref, grow_ref, ng_ref, out_ref,
                       c_ref, n_ref, m_ref):
    L = q_ref.shape[0]

    @pl.when(pl.program_id(1) == 0)
    def _():
        c_ref[...] = jnp.zeros_like(c_ref)
        n_ref[...] = jnp.zeros_like(n_ref)
        m_ref[...] = jnp.zeros_like(m_ref)

    row = lax.broadcasted_iota(jnp.int32, (L, L), 0)
    col = lax.broadcasted_iota(jnp.int32, (L, L), 1)
    tri = row >= col
    for h in range(M_HEADS):
        hs = slice(h * M_DK, (h + 1) * M_DK)
        q = q_ref[:, hs]
        k = k_ref[:, hs]
        v = v_ref[:, hs]
        ig_row = grow_ref[0, h:h + 1, :]
        lf_row = _log_sigmoid(grow_ref[0, M_HEADS + h:M_HEADS + h + 1, :])
        ig_col = gcol_ref[:, h:h + 1]
        lf_col = _log_sigmoid(gcol_ref[:, M_HEADS + h:M_HEADS + h + 1])
        bcum_col = jnp.sum(jnp.where(tri, lf_row, 0.0), axis=1, keepdims=True)
        bcum_row = jnp.sum(jnp.where(row <= col, lf_col, 0.0), axis=0, keepdims=True)
        a_row = ig_row - bcum_row
        a_col = ig_col - bcum_col
        m_prev = m_ref[h, 0:1, 0:1]
        big_m = jnp.maximum(m_prev, jnp.max(jnp.where(tri, a_row, -jnp.inf), axis=1, keepdims=True))
        big_m_last = jnp.maximum(m_prev, jnp.max(a_row, axis=1, keepdims=True))
        w = jnp.where(tri, jnp.exp(a_row - big_m), 0.0)
        s_inter = jnp.exp(m_prev - big_m)
        qk = lax.dot_general(q, k, (((1,), (1,)), ((), ())), preferred_element_type=F32)
        a = qk * w
        c_old = c_ref[h]
        n_old = n_ref[h]
        num = s_inter * jnp.dot(q, c_old.astype(BF16), preferred_element_type=F32) \
            + jnp.dot(a.astype(BF16), v, preferred_element_type=F32)
        den = s_inter * jnp.sum(q.astype(F32) * n_old, axis=1, keepdims=True) \
            + jnp.sum(a, axis=1, keepdims=True)
        m_t = bcum_col + big_m
        hh = num / jnp.maximum(jnp.abs(den), jnp.exp(-m_t))
        b_last = jnp.sum(lf_row, axis=1, keepdims=True)
        decay = jnp.exp(m_prev - big_m_last)
        kw = k.astype(F32) * jnp.exp(a_col - big_m_last)
        c_ref[h] = decay * c_old + lax.dot_general(
            kw.astype(BF16), v, (((0,), (0,)), ((), ())), preferred_element_type=F32)
        n_ref[h] = decay * n_old + jnp.sum(kw, axis=0, keepdims=True)
        m_ref[h] = jnp.broadcast_to(b_last + big_m_last, m_ref.shape[1:])
        hn = hh * lax.rsqrt(jnp.mean(hh * hh, axis=1, keepdims=True) + LN_EPS) * ng_ref[:, hs]
        hn = hn * _sigmoid(o_ref[:, hs].astype(F32))
        out_ref[:, hs] = hn.astype(BF16)


def _mlstm_core_call(qkvo, gcol, grow, norm_g):
    d = D_MODEL
    L = MLSTM_L
    nc = SEQ // L
    row_spec = lambda j: pl.BlockSpec((L, d), lambda b, c, j=j: (b * nc + c, j))
    return pl.pallas_call(
        _mlstm_core_kernel,
        grid=(BATCH, nc),
        in_specs=[
            row_spec(0), row_spec(1), row_spec(2), row_spec(3),
            pl.BlockSpec((L, LANES), lambda b, c: (b * nc + c, 0)),
            pl.BlockSpec((1, SUBLANES, L), lambda b, c: (b, 0, c)),
            _const_spec((1, d)),
        ],
        out_specs=pl.BlockSpec((L, d), lambda b, c: (b * nc + c, 0)),
        out_shape=jax.ShapeDtypeStruct((N_TOK, d), BF16),
        name="mlstm_core",
        scratch_shapes=[
            pltpu.VMEM((M_HEADS, M_DK, M_DV), F32),
            pltpu.VMEM((M_HEADS, 1, M_DK), F32),
            pltpu.VMEM((M_HEADS, SUBLANES, LANES), F32),
        ],
        compiler_params=pltpu.CompilerParams(
            dimension_semantics=("arbitrary", "arbitrary"), vmem_limit_bytes=VMEM_LIMIT),
    )(qkvo, qkvo, qkvo, qkvo, gcol, grow, norm_g)


def _layer_norm_rows(z, g, b):
    mu = jnp.mean(z, axis=1, keepdims=True)
    zc = z - mu
    var = jnp.mean(zc * zc, axis=1, keepdims=True)
    return zc * lax.rsqrt(var + LN_EPS) * g + b


def _outproj_kernel(a_ref, w_ref, x_ref, ada_ref, lng_ref, lnb_ref, o_ref, *, gate_idx):
    d = D_MODEL
    gate = ada_ref[:, gate_idx * d:(gate_idx + 1) * d]
    y = jnp.dot(a_ref[...], w_ref[...], preferred_element_type=F32)
    z = DEEPNORM_ALPHA * x_ref[...] + (1.0 + gate) * y
    o_ref[...] = _layer_norm_rows(z, lng_ref[...], lnb_ref[...])


def _outproj_call(a, w, x2, ada_l, ln_g, ln_b, gate_idx):
    d = D_MODEL
    tm = ROW_TILE
    nt = SEQ // tm
    return pl.pallas_call(
        functools.partial(_outproj_kernel, gate_idx=gate_idx),
        grid=(BATCH, nt),
        in_specs=[
            pl.BlockSpec((tm, d), lambda b, s: (b * nt + s, 0)),
            _const_spec((d, d)),
            pl.BlockSpec((tm, d), lambda b, s: (b * nt + s, 0)),
            pl.BlockSpec((None, 1, 6 * d), lambda b, s: (b, 0, 0)),
            _const_spec((1, d)),
            _const_spec((1, d)),
        ],
        out_specs=pl.BlockSpec((tm, d), lambda b, s: (b * nt + s, 0)),
        out_shape=jax.ShapeDtypeStruct((N_TOK, d), F32),
        name="outproj_ln",
        compiler_params=pltpu.CompilerParams(
            dimension_semantics=("arbitrary", "arbitrary"), vmem_limit_bytes=VMEM_LIMIT),
    )(a, w, x2, ada_l, ln_g, ln_b)


def _ffn_kernel(x_ref, ada_ref, wup_ref, cw_ref, cb_ref, wdn_ref, lng_ref, lnb_ref, o_ref,
                carry_ref, buf_ref, acc_ref):
    d = D_MODEL
    tm = x_ref.shape[0]

    @pl.when(pl.program_id(1) == 0)
    def _():
        carry_ref[...] = jnp.zeros_like(carry_ref)

    sh = ada_ref[:, 3 * d:4 * d]
    sc = ada_ref[:, 4 * d:5 * d]
    gate = ada_ref[:, 5 * d:6 * d]
    x = x_ref[...]
    u = (x * (1.0 + sc) + sh).astype(BF16)
    acc_ref[...] = jnp.zeros_like(acc_ref)

    def chunk(c, carry):
        def conv_part(part):
            p = jnp.dot(u, wup_ref[part, c], preferred_element_type=F32)
            y = _causal_conv(p, carry_ref[part, c], buf_ref, cw_ref[part, c], cb_ref[part, c], FFN_CONV)
            carry_ref[part, c] = p[tm - SUBLANES:tm, :]
            return y
        a = conv_part(0)
        g = conv_part(1)
        act = 0.5 * g * (1.0 + lax.erf(g * math.sqrt(0.5))) * a
        acc_ref[...] += jnp.dot(act.astype(BF16), wdn_ref[c], preferred_element_type=F32)
        return carry

    lax.fori_loop(0, FFN_NCH, chunk, 0)
    z = DEEPNORM_ALPHA * x + (1.0 + gate) * acc_ref[...]
    o_ref[...] = _layer_norm_rows(z, lng_ref[...], lnb_ref[...])


def _ffn_call(x2, ada_l, wup, cw, cb, wdn, ln_g, ln_b):
    d = D_MODEL
    tm = ROW_TILE
    nt = SEQ // tm
    return pl.pallas_call(
        _ffn_kernel,
        grid=(BATCH, nt),
        in_specs=[
            pl.BlockSpec((tm, d), lambda b, s: (b * nt + s, 0)),
            pl.BlockSpec((None, 1, 6 * d), lambda b, s: (b, 0, 0)),
            _const_spec((2, FFN_NCH, d, FFN_FC)),
            _const_spec((2, FFN_NCH, FFN_CONV, FFN_FC)),
            _const_spec((2, FFN_NCH, 1, FFN_FC)),
            _const_spec((FFN_NCH, FFN_FC, d)),
            _const_spec((1, d)),
            _const_spec((1, d)),
        ],
        out_specs=pl.BlockSpec((tm, d), lambda b, s: (b * nt + s, 0)),
        out_shape=jax.ShapeDtypeStruct((N_TOK, d), F32),
        name="conv_ffn_ln",
        scratch_shapes=[
            pltpu.VMEM((2, FFN_NCH, SUBLANES, FFN_FC), F32),
            pltpu.VMEM((tm + SUBLANES, FFN_FC), F32),
            pltpu.VMEM((tm, d), F32),
        ],
        compiler_params=pltpu.CompilerParams(
            dimension_semantics=("arbitrary", "arbitrary"), vmem_limit_bytes=VMEM_LIMIT),
    )(x2, ada_l, wup, cw, cb, wdn, ln_g, ln_b)


def _ffn_layer(x2, ada_l, w_up, conv_w, conv_b, w_down, ln_g, ln_b):
    d = D_MODEL
    wup = w_up.astype(BF16).reshape(d, 2, FFN_NCH, FFN_FC).transpose(1, 2, 0, 3)
    cw = conv_w.reshape(FFN_CONV, 2, FFN_NCH, FFN_FC).transpose(1, 2, 0, 3)
    cb = conv_b.reshape(2, FFN_NCH, 1, FFN_FC)
    wdn = w_down.astype(BF16).reshape(FFN_NCH, FFN_FC, d)
    return _ffn_call(x2, ada_l, wup, cw, cb, wdn, ln_g.reshape(1, d), ln_b.reshape(1, d))


def _diff_inproj_kernel(x_ref, ada_ref, w_ref, o_ref):
    d = D_MODEL
    cc = 512
    sh = ada_ref[:, 0:d]
    sc = ada_ref[:, d:2 * d]
    u = (x_ref[...] * (1.0 + sc) + sh).astype(BF16)
    for c in range(3 * d // cc):
        cols = slice(c * cc, (c + 1) * cc)
        p = jnp.dot(u, w_ref[:, cols], preferred_element_type=F32)
        if c * cc < d:
            p = p * (D_HD ** -0.5)
        o_ref[:, cols] = p.astype(BF16)


def _diff_inproj_call(x2, ada_l, w):
    d = D_MODEL
    tm = ROW_TILE
    nt = SEQ // tm
    return pl.pallas_call(
        _diff_inproj_kernel,
        grid=(BATCH, nt),
        in_specs=[
            pl.BlockSpec((tm, d), lambda b, s: (b * nt + s, 0)),
            pl.BlockSpec((None, 1, 6 * d), lambda b, s: (b, 0, 0)),
            _const_spec((d, 3 * d)),
        ],
        out_specs=pl.BlockSpec((tm, 3 * d), lambda b, s: (b * nt + s, 0)),
        out_shape=jax.ShapeDtypeStruct((N_TOK, 3 * d), BF16),
        name="diff_inproj",
        compiler_params=pltpu.CompilerParams(
            dimension_semantics=("arbitrary", "arbitrary"), vmem_limit_bytes=VMEM_LIMIT),
    )(x2, ada_l, w)


def _t5_bucket_np(rel):
    nb = REL_BUCKETS // 2
    max_exact = nb // 2
    n = -rel
    ret = jnp.where(n < 0, nb, 0)
    n = jnp.abs(n)
    nf = jnp.maximum(n, 1).astype(jnp.float32)
    large = max_exact + (jnp.log(nf / max_exact) / math.log(REL_MAX_DIST / max_exact)
                         * (nb - max_exact)).astype(jnp.int32)
    large = jnp.minimum(large, nb - 1)
    return ret + jnp.where(n < max_exact, n, large)


FAR_BUCKET = REL_BUCKETS // 2 - 1


def _bias_kernel(tab_ref, idx_ref, o_ref):
    h = pl.program_id(0)
    idx = idx_ref[...]
    far = tab_ref[FAR_BUCKET, h]
    acc = jnp.full(idx.shape, -jnp.inf, F32)
    for b in range(REL_BUCKETS):
        acc = jnp.where(idx == b, tab_ref[b, h] - far, acc)
    o_ref[0] = acc


def _bias_tiles(rel_bias):
    tq, tk = ATT_TQ, ATT_TK
    qpos = jnp.arange(tq)[:, None]
    kpos = jnp.arange(tk)[None, :]
    diag = _t5_bucket_np(kpos - qpos)
    diag = jnp.where((kpos // CHUNK) <= (qpos // CHUNK), diag, -1)
    sub = _t5_bucket_np(kpos - tk - qpos)
    idx = jnp.stack([diag, sub]).astype(jnp.int32)
    return pl.pallas_call(
        _bias_kernel,
        grid=(D_HEADS,),
        in_specs=[
            pl.BlockSpec(memory_space=pltpu.SMEM),
            pl.BlockSpec((2, tq, tk), lambda h: (0, 0, 0)),
        ],
        out_specs=pl.BlockSpec((1, 2, tq, tk), lambda h: (h, 0, 0, 0)),
        out_shape=jax.ShapeDtypeStruct((D_HEADS, 2, tq, tk), F32),
        name="rel_bias_tiles",
    )(rel_bias, idx)


def _attn_kernel(q_ref, k_ref, v_ref, bias_ref, lam_ref, ng_ref, o_ref, m_ref, l_ref, acc_ref,
                 *, lambda_init):
    tq = q_ref.shape[0]
    tk = ATT_TK
    qi = pl.program_id(2)
    q = q_ref[...]
    lane = lax.broadcasted_iota(jnp.int32, q.shape, 1)
    zero = jnp.zeros_like(q)
    qz = jnp.concatenate([jnp.where(lane < D_HD, q, zero), jnp.where(lane >= D_HD, q, zero)], axis=0)
    m_ref[...] = jnp.full_like(m_ref, NEG_BIG)
    l_ref[...] = jnp.zeros_like(l_ref)
    acc_ref[...] = jnp.zeros_like(acc_ref)

    def step(k0, bias):
        k = k_ref[pl.ds(k0, tk), :]
        v = v_ref[pl.ds(k0, tk), :]
        s = lax.dot_general(qz, k, (((1,), (1,)), ((), ())), preferred_element_type=F32)
        if bias is not None:
            s = s + jnp.concatenate([bias, bias], axis=0)
        m_prev = m_ref[...]
        m_new = jnp.maximum(m_prev, jnp.max(s, axis=1, keepdims=True))
        alpha = jnp.exp(m_prev - m_new)
        p = jnp.exp(s - m_new)
        l_ref[...] = alpha * l_ref[...] + jnp.sum(p, axis=1, keepdims=True)
        acc_ref[...] = alpha * acc_ref[...] + jnp.dot(p.astype(BF16), v, preferred_element_type=F32)
        m_ref[...] = m_new

    def far_body(j, carry):
        step(pl.multiple_of(j * tk, tk), None)
        return carry

    lax.fori_loop(0, qi - 1, far_body, 0)

    @pl.when(qi >= 1)
    def _():
        step(pl.multiple_of((qi - 1) * tk, tk), bias_ref[0, 1])

    step(pl.multiple_of(qi * tk, tk), bias_ref[0, 0])

    lam = lam_ref[...]
    lam_full = jnp.exp(jnp.sum(lam[0:1] * lam[1:2], axis=1, keepdims=True)) \
        - jnp.exp(jnp.sum(lam[2:3] * lam[3:4], axis=1, keepdims=True)) + lambda_init
    o = acc_ref[...] / l_ref[...]
    out = o[0:tq] - lam_full * o[tq:2 * tq]
    out = out * lax.rsqrt(jnp.mean(out * out, axis=1, keepdims=True) + LN_EPS)
    o_ref[...] = (out * ng_ref[...] * (1.0 - lambda_init)).astype(BF16)


def _attn_call(qkv, bias, lam, norm_g, lambda_init):
    tq = ATT_TQ
    nq = SEQ // tq
    return pl.pallas_call(
        functools.partial(_attn_kernel, lambda_init=lambda_init),
        grid=(BATCH, D_HEADS, nq),
        in_specs=[
            pl.BlockSpec((tq, D_VD), lambda b, h, i: (b * nq + i, h)),
            pl.BlockSpec((SEQ, D_VD), lambda b, h, i: (b, D_HEADS + h)),
            pl.BlockSpec((SEQ, D_VD), lambda b, h, i: (b, 2 * D_HEADS + h)),
            pl.BlockSpec((1, 2, tq, ATT_TK), lambda b, h, i: (h, 0, 0, 0)),
            pl.BlockSpec((4, D_HD), lambda b, h, i: (0, 0)),
            pl.BlockSpec((1, D_VD), lambda b, h, i: (0, h)),
        ],
        out_specs=pl.BlockSpec((tq, D_VD), lambda b, h, i: (b * nq + i, h)),
        out_shape=jax.ShapeDtypeStruct((N_TOK, D_MODEL), BF16),
        name="diff_attn",
        scratch_shapes=[
            pltpu.VMEM((2 * tq, 1), F32),
            pltpu.VMEM((2 * tq, 1), F32),
            pltpu.VMEM((2 * tq, D_VD), F32),
        ],
        compiler_params=pltpu.CompilerParams(
            dimension_semantics=("arbitrary", "arbitrary", "arbitrary"), vmem_limit_bytes=VMEM_LIMIT),
    )(qkv, qkv, qkv, bias, lam, norm_g)


def _lambda_init(layer_idx):
    return 0.8 - 0.6 * math.exp(-0.3 * layer_idx)


def kernel(x, c, w_ada, b_ada, ln_g, ln_b, m_w_in, m_b_gate, m_conv_w, m_conv_b, m_norm_g, m_w_out,
           d_w_in, d_lambda, d_norm_g, d_w_out, rel_bias, f_w_up, f_conv_w, f_conv_b, f_w_down):
    d = D_MODEL
    x2 = x.reshape(N_TOK, d)
    ada = _ada_call(c, w_ada, b_ada).reshape(DEPTH, BATCH, 1, 6 * d)
    for i in range(DEPTH):
        ada_l = ada[i]
        j = i // 2
        if i % 2 == 0:
            w_in = m_w_in[j]
            n_gate = 2 * M_HEADS
            wqk = w_in[:, :2 * d].astype(BF16)
            wvo = w_in[:, 2 * d:4 * d].astype(BF16)
            wg = jnp.pad(w_in[:, 4 * d:], ((0, 0), (0, LANES - n_gate))).astype(BF16)
            bg = jnp.pad(m_b_gate[j], (0, LANES - n_gate)).reshape(1, LANES)
            qkvo, gcol, grow = _mlstm_inproj_call(
                x2, ada_l, wqk, wvo, wg, bg, m_conv_w[j], m_conv_b[j].reshape(1, 2 * d))
            mixed = _mlstm_core_call(qkvo, gcol, grow, m_norm_g[j].reshape(1, d))
            w_out = m_w_out[j]
        else:
            qkv = _diff_inproj_call(x2, ada_l, d_w_in[j].astype(BF16))
            bias = _bias_tiles(rel_bias)
            mixed = _attn_call(qkv, bias, d_lambda[j], d_norm_g[j].reshape(1, d), _lambda_init(i))
            w_out = d_w_out[j]
        x2 = _outproj_call(mixed, w_out.astype(BF16), x2, ada_l,
                           ln_g[i, 0].reshape(1, d), ln_b[i, 0].reshape(1, d), gate_idx=2)
        x2 = _ffn_layer(x2, ada_l, f_w_up[i], f_conv_w[i], f_conv_b[i], f_w_down[i],
                        ln_g[i, 1], ln_b[i, 1])
    return x2.reshape(BATCH, SEQ, d)
```
